```python
import jax, jax.numpy as jnp
from jax import lax
import numpy as np

D_MODEL = 1024
BATCH = 4
SEQ = 8192
DEPTH = 4
DEC_BATCH = 16
DEC_SEQ = 64
PAST_LEN = 1024

CHUNK = 64
HEAD_DIM = D_MODEL // 16
D_A = 3 * D_MODEL // 8
D_B = 3 * D_MODEL // 8
D_C = D_MODEL - D_A - D_B
N_HEADS_B = D_B // HEAD_DIM
D_IN = 2 * D_A + 2 * D_B + 3 * D_C
CONV_A_WIDTH = 31
CONV_C_WIDTH = 3
GMLP_BLOCK = 128
D_FF = 2816
ALPHA = (2.0 * DEPTH) ** 0.25
BETA = (8.0 * DEPTH) ** -0.25
LN_EPS = 1e-5

kernel_name = "hybrid_streaming_conv_gmlp_encoder_step"


def layer_norm(x, g, b):
    xf = x.astype(jnp.float32)
    mu = jnp.mean(xf, axis=-1, keepdims=True)
    var = jnp.mean(jnp.square(xf - mu), axis=-1, keepdims=True)
    return ((xf - mu) * lax.rsqrt(var + LN_EPS) * g.astype(jnp.float32) + b.astype(jnp.float32)).astype(x.dtype)


def swiglu_ffn(x, w_in, w_out):
    gate, up = jnp.split(x @ w_in, 2, axis=-1)
    return (jax.nn.silu(gate) * up) @ w_out


def depthwise_causal_conv(x_ext, w):
    k, c = w.shape
    return lax.conv_general_dilated(
        x_ext, w[:, None, :].astype(x_ext.dtype), window_strides=(1,), padding='VALID',
        dimension_numbers=('NWC', 'WIO', 'NWC'), feature_group_count=c)


def chunk_causal_mask(length):
    pos = jnp.arange(length)
    return (pos[None, :] // CHUNK) <= (pos[:, None] // CHUNK)


def spatial_gating(u, v, w_s, b_s, block_len):
    bt, t, _ = v.shape
    w = jnp.where(chunk_causal_mask(block_len), w_s[:, :block_len, :block_len], 0.0).astype(v.dtype)
    vh = v.reshape(bt, t // block_len, block_len, N_HEADS_B, HEAD_DIM)
    s = jnp.einsum('hij,bnjhd->bnihd', w, vh) + b_s[:, :block_len].T[None, None, :, :, None]
    return u * s.reshape(bt, t, D_B)


def token_mixers(h, hist_a, hist_c, w_in, conv_a_w, conv_a_b, norm_a_g, norm_a_b,
                 norm_b_g, norm_b_b, w_s, b_s, conv_c_w, w_out, block_len):
    z = h @ w_in
    cuts = [D_A, 2 * D_A, 2 * D_A + D_B, 2 * D_A + 2 * D_B,
            2 * D_A + 2 * D_B + D_C, 2 * D_A + 2 * D_B + 2 * D_C]
    p_a, g_a, u_b, v_b, x_c, gate_b, gate_c = jnp.split(z, cuts, axis=-1)
    a = p_a * jax.nn.sigmoid(g_a)
    a_ext = jnp.concatenate([hist_a, a], axis=1)
    a_conv = depthwise_causal_conv(a_ext, conv_a_w) + conv_a_b
    out_a = jax.nn.silu(layer_norm(a_conv, norm_a_g, norm_a_b))
    new_hist_a = a_ext[:, -(CONV_A_WIDTH - 1):]
    v_n = layer_norm(v_b, norm_b_g, norm_b_b)
    out_b = spatial_gating(u_b, v_n, w_s, b_s, block_len)
    cx = gate_c * x_c
    c_ext = jnp.concatenate([hist_c, cx], axis=1)
    out_c = gate_b * depthwise_causal_conv(c_ext, conv_c_w)
    new_hist_c = c_ext[:, -(CONV_C_WIDTH - 1):]
    y = jnp.concatenate([out_a, out_b, out_c], axis=-1) @ w_out
    return y, new_hist_a, new_hist_c, v_n


def trunk(x, hist_a, hist_c, block_len, weights):
    (w_ffn1_in, w_ffn1_out, ln1_g, ln1_b, w_in, conv_a_w, conv_a_b, norm_a_g, norm_a_b,
     norm_b_g, norm_b_b, w_s, b_s, conv_c_w, w_out, ln2_g, ln2_b,
     w_ffn2_in, w_ffn2_out, ln3_g, ln3_b) = weights
    states_a, states_c, states_v = [], [], []
    for l in range(DEPTH):
        x = layer_norm(ALPHA * x + 0.5 * swiglu_ffn(x, w_ffn1_in[l], w_ffn1_out[l]), ln1_g[l], ln1_b[l])
        y, ha, hc, vn = token_mixers(x, hist_a[l], hist_c[l], w_in[l], conv_a_w[l], conv_a_b[l],
                                     norm_a_g[l], norm_a_b[l], norm_b_g[l], norm_b_b[l],
                                     w_s[l], b_s[l], conv_c_w[l], w_out[l], block_len)
        x = layer_norm(ALPHA * x + y, ln2_g[l], ln2_b[l])
        x = layer_norm(ALPHA * x + 0.5 * swiglu_ffn(x, w_ffn2_in[l], w_ffn2_out[l]), ln3_g[l], ln3_b[l])
        states_a.append(ha)
        states_c.append(hc)
        states_v.append(vn)
    return x, jnp.stack(states_a), jnp.stack(states_c), jnp.stack(states_v)


def setup_inputs(seed: int = 0) -> dict:
    key = jax.random.key(seed)
    ks = iter(jax.random.split(key, 32))
    nrm = lambda shape, s: s * jax.random.normal(next(ks), shape, jnp.float32)
    L = DEPTH
    return {
        "x_prompt": nrm((BATCH, SEQ, D_MODEL), 1.0),
        "x_sample": nrm((DEC_BATCH, DEC_SEQ, D_MODEL), 1.0),
        "cache_conv_a": nrm((L, DEC_BATCH, CONV_A_WIDTH - 1, D_A), 0.5),
        "cache_conv_c": nrm((L, DEC_BATCH, CONV_C_WIDTH - 1, D_C), 0.5),
        "w_ffn1_in": nrm((L, D_MODEL, 2 * D_FF), D_MODEL ** -0.5),
        "w_ffn1_out": nrm((L, D_FF, D_MODEL), BETA * D_FF ** -0.5),
        "ln1_g": 1.0 + nrm((L, D_MODEL), 0.01),
        "ln1_b": nrm((L, D_MODEL), 0.01),
        "w_in": nrm((L, D_MODEL, D_IN), D_MODEL ** -0.5),
        "conv_a_w": nrm((L, CONV_A_WIDTH, D_A), CONV_A_WIDTH ** -0.5),
        "conv_a_b": nrm((L, D_A), 0.01),
        "norm_a_g": 1.0 + nrm((L, D_A), 0.01),
        "norm_a_b": nrm((L, D_A), 0.01),
        "norm_b_g": 1.0 + nrm((L, D_B), 0.01),
        "norm_b_b": nrm((L, D_B), 0.01),
        "w_s": nrm((L, N_HEADS_B, GMLP_BLOCK, GMLP_BLOCK), 0.5 * GMLP_BLOCK ** -0.5),
        "b_s": 1.0 + nrm((L, N_HEADS_B, GMLP_BLOCK), 0.01),
        "conv_c_w": nrm((L, CONV_C_WIDTH, D_C), CONV_C_WIDTH ** -0.5),
        "w_out": nrm((L, D_MODEL, D_MODEL), BETA * D_MODEL ** -0.5),
        "ln2_g": 1.0 + nrm((L, D_MODEL), 0.01),
        "ln2_b": nrm((L, D_MODEL), 0.01),
        "w_ffn2_in": nrm((L, D_MODEL, 2 * D_FF), D_MODEL ** -0.5),
        "w_ffn2_out": nrm((L, D_FF, D_MODEL), BETA * D_FF ** -0.5),
        "ln3_g": 1.0 + nrm((L, D_MODEL), 0.01),
        "ln3_b": nrm((L, D_MODEL), 0.01),
    }


def reference(x_prompt, x_sample, cache_conv_a, cache_conv_c, w_ffn1_in, w_ffn1_out, ln1_g, ln1_b,
              w_in, conv_a_w, conv_a_b, norm_a_g, norm_a_b, norm_b_g, norm_b_b, w_s, b_s,
              conv_c_w, w_out, ln2_g, ln2_b, w_ffn2_in, w_ffn2_out, ln3_g, ln3_b):
    weights = (w_ffn1_in, w_ffn1_out, ln1_g, ln1_b, w_in, conv_a_w, conv_a_b, norm_a_g, norm_a_b,
               norm_b_g, norm_b_b, w_s, b_s, conv_c_w, w_out, ln2_g, ln2_b,
               w_ffn2_in, w_ffn2_out, ln3_g, ln3_b)
    bp = x_prompt.shape[0]
    zeros_a = jnp.zeros((DEPTH, bp, CONV_A_WIDTH - 1, D_A), x_prompt.dtype)
    zeros_c = jnp.zeros((DEPTH, bp, CONV_C_WIDTH - 1, D_C), x_prompt.dtype)
    y_prompt, st_a_p, st_c_p, _ = trunk(x_prompt, zeros_a, zeros_c, GMLP_BLOCK, weights)
    y_sample, st_a_s, st_c_s, st_v_s = trunk(x_sample, cache_conv_a, cache_conv_c, x_sample.shape[1], weights)
    return (y_prompt, y_sample, st_a_p, st_c_p, st_a_s, st_c_s, st_v_s)
```

```python
import functools

import jax
import jax.numpy as jnp
from jax import lax
from jax.experimental import pallas as pl
from jax.experimental.pallas import tpu as pltpu

D_MODEL = 1024
DEPTH = 4
CHUNK = 64
HEAD_DIM = 64
D_A = 384
D_B = 384
D_C = 256
N_HEADS_B = D_B // HEAD_DIM
D_IN = 2 * D_A + 2 * D_B + 3 * D_C
CONV_A_WIDTH = 31
CONV_C_WIDTH = 3
HIST_A = CONV_A_WIDTH - 1
HIST_C = CONV_C_WIDTH - 1
GMLP_BLOCK = 128
D_FF = 2816
ALPHA = (2.0 * DEPTH) ** 0.25
LN_EPS = 1e-5

V7X_SUBLANES = 8
V7X_MXU_COLS = 256
V7X_VMEM_BYTES = 64 * 1024 * 1024

FFN_ROWS = 512
FF_CHUNK = V7X_MXU_COLS
MIX_ROWS = 512
CONV_ROWS = 64
A_PAD = 32
C_PAD = 8

F32 = jnp.float32
BF16 = jnp.bfloat16


def _layer_norm(y, g, b):
    mu = jnp.mean(y, axis=-1, keepdims=True)
    d = y - mu
    var = jnp.mean(d * d, axis=-1, keepdims=True)
    return d * lax.rsqrt(var + LN_EPS) * g + b


def _vmem_limit(block_bytes, scratch_bytes, temp_bytes):
    need = block_bytes + scratch_bytes + temp_bytes
    assert need <= V7X_VMEM_BYTES, need
    return need


def _ffn_kernel(x_ref, w_in_ref, w_out_ref, g_ref, b_ref, o_ref, act_ref):
    x = x_ref[...]
    xb = x.astype(BF16)
    for c in range(D_FF // FF_CHUNK):
        gu = jnp.dot(xb, w_in_ref[:, 2 * c * FF_CHUNK:2 * (c + 1) * FF_CHUNK],
                     preferred_element_type=F32)
        gate = gu[:, :FF_CHUNK]
        up = gu[:, FF_CHUNK:]
        act_ref[:, c * FF_CHUNK:(c + 1) * FF_CHUNK] = (jax.nn.silu(gate) * up).astype(BF16)
    y = jnp.dot(act_ref[...], w_out_ref[...], preferred_element_type=F32)
    o_ref[...] = _layer_norm(ALPHA * x + 0.5 * y, g_ref[...], b_ref[...])


def _ffn(x2d, w_in_b, w_out_b, ln_g, ln_b, layer):
    rows = x2d.shape[0]
    tm = min(FFN_ROWS, rows)
    assert rows % tm == 0
    resident = dict(pipeline_mode=pl.Buffered(1))
    block_bytes = (2 * 2 * tm * D_MODEL * 4
                   + D_MODEL * 2 * D_FF * 2 + D_FF * D_MODEL * 2
                   + 2 * 2 * D_MODEL * 4)
    scratch_bytes = tm * D_FF * 2
    temp_bytes = tm * (D_MODEL * 2 + 2 * FF_CHUNK * 4 * 3 + D_MODEL * 4 * 3)
    return pl.pallas_call(
        _ffn_kernel,
        grid=(rows // tm,),
        in_specs=[
            pl.BlockSpec((tm, D_MODEL), lambda i: (i, 0)),
            pl.BlockSpec((None, D_MODEL, 2 * D_FF), lambda i: (layer, 0, 0), **resident),
            pl.BlockSpec((None, D_FF, D_MODEL), lambda i: (layer, 0, 0), **resident),
            pl.BlockSpec((None, 1, D_MODEL), lambda i: (layer, 0, 0)),
            pl.BlockSpec((None, 1, D_MODEL), lambda i: (layer, 0, 0)),
        ],
        out_specs=pl.BlockSpec((tm, D_MODEL), lambda i: (i, 0)),
        out_shape=jax.ShapeDtypeStruct((rows, D_MODEL), F32),
        scratch_shapes=[pltpu.VMEM((tm, D_FF), BF16)],
        compiler_params=pltpu.CompilerParams(
            dimension_semantics=("arbitrary",),
            vmem_limit_bytes=_vmem_limit(block_bytes, scratch_bytes, temp_bytes)),
        name="ffn",
    )(x2d, w_in_b, w_out_b, ln_g, ln_b)


def _mixer_kernel(h_ref, ha0_ref, hc0_ref, w_in_ref, cw_a_ref, cb_a_ref, na_g_ref, na_b_ref,
                  nb_g_ref, nb_b_ref, wcat_ref, bias_ref, cw_c_ref, w_out_ref, g2_ref, b2_ref,
                  *rest, nseq, length, block_len, emit_v):
    if emit_v:
        o_ref, sa_ref, sc_ref, v_ref, abuf, cbuf, mix_ref = rest
    else:
        o_ref, sa_ref, sc_ref, abuf, cbuf, mix_ref = rest
    rows = nseq * length
    a_lo = A_PAD - HIST_A
    c_lo = C_PAD - HIST_C

    @pl.when(pl.program_id(1) == 0)
    def _():
        abuf[:, a_lo:A_PAD, :] = ha0_ref[...]
        cbuf[:, c_lo:C_PAD, :] = hc0_ref[...]

    @pl.when(pl.program_id(1) != 0)
    def _():
        abuf[:, a_lo:A_PAD, :] = abuf[:, length + a_lo:length + A_PAD, :]
        cbuf[:, c_lo:C_PAD, :] = cbuf[:, length + c_lo:length + C_PAD, :]

    h = h_ref[...].reshape(rows, D_MODEL)
    hb = h.astype(BF16)

    za = jnp.dot(hb, w_in_ref[:, 0:2 * D_A], preferred_element_type=F32)
    a = za[:, :D_A] * jax.nn.sigmoid(za[:, D_A:])
    abuf[:, A_PAD:A_PAD + length, :] = a.reshape(nseq, length, D_A)
    sa_ref[...] = abuf[:, length + a_lo:length + A_PAD, :]
    cb_a = cb_a_ref[...]
    for s in range(nseq):
        for r0 in range(0, length, CONV_ROWS):
            acc = jnp.broadcast_to(cb_a, (CONV_ROWS, D_A))
            for k in range(CONV_A_WIDTH):
                acc = acc + abuf[s, a_lo + r0 + k:a_lo + r0 + k + CONV_ROWS, :] * cw_a_ref[k:k + 1, :]
            out_a = jax.nn.silu(_layer_norm(acc, na_g_ref[...], na_b_ref[...]))
            mix_ref[s * length + r0:s * length + r0 + CONV_ROWS, 0:D_A] = out_a.astype(BF16)

    zb = jnp.dot(hb, w_in_ref[:, 2 * D_A:2 * D_A + 2 * D_B], preferred_element_type=F32)
    vn = _layer_norm(zb[:, D_B:], nb_g_ref[...], nb_b_ref[...])
    if emit_v:
        v_ref[...] = vn.reshape(nseq, length, D_B)
    wi = lax.broadcasted_iota(jnp.int32, (block_len, N_HEADS_B * block_len), 0)
    wj = lax.broadcasted_iota(jnp.int32, (block_len, N_HEADS_B * block_len), 1) % block_len
    wm = jnp.where(wj // CHUNK <= wi // CHUNK, wcat_ref[...], 0.0).astype(BF16)
    lane_head = lax.broadcasted_iota(jnp.int32, (block_len, D_B), 1) // HEAD_DIM
    for blk in range(rows // block_len):
        lo = blk * block_len
        vb = vn[lo:lo + block_len]
        vstack = jnp.concatenate(
            [jnp.where(lane_head == hh, vb, 0.0) for hh in range(N_HEADS_B)], axis=0).astype(BF16)
        sg = jnp.dot(wm, vstack, preferred_element_type=F32) + bias_ref[...]
        mix_ref[lo:lo + block_len, D_A:D_A + D_B] = (zb[lo:lo + block_len, :D_B] * sg).astype(BF16)

    zc = jnp.dot(hb, w_in_ref[:, 2 * D_A + 2 * D_B:D_IN], preferred_element_type=F32)
    cx = zc[:, 2 * D_C:] * zc[:, :D_C]
    cbuf[:, C_PAD:C_PAD + length, :] = cx.reshape(nseq, length, D_C)
    sc_ref[...] = cbuf[:, length + c_lo:length + C_PAD, :]
    conv_c = cbuf[:, c_lo:c_lo + length, :] * cw_c_ref[0:1, :]
    for k in range(1, CONV_C_WIDTH):
        conv_c = conv_c + cbuf[:, c_lo + k:c_lo + k + length, :] * cw_c_ref[k:k + 1, :]
    out_c = zc[:, D_C:2 * D_C] * conv_c.reshape(rows, D_C)
    mix_ref[:, D_A + D_B:D_MODEL] = out_c.astype(BF16)

    y = jnp.dot(mix_ref[...], w_out_ref[...], preferred_element_type=F32)
    o = _layer_norm(ALPHA * h + y, g2_ref[...], b2_ref[...])
    o_ref[...] = o.reshape(nseq, length, D_MODEL)


def _mixer(h, hist_a, hist_c, w_in_b, conv_a_w, conv_a_b, norm_a_g, norm_a_b, norm_b_g, norm_b_b,
           wcat, bias, conv_c_w, w_out_b, ln_g, ln_b, layer, *, nseq, length, block_len, emit_v):
    batch, seq, _ = h.shape
    assert batch % nseq == 0 and seq % length == 0 and length % block_len == 0
    assert length % CONV_ROWS == 0 and length >= HIST_A
    rows = nseq * length

    def per_layer(shape):
        return pl.BlockSpec((None,) + shape, lambda b, t: (layer,) + (0,) * len(shape))

    in_specs = [
        pl.BlockSpec((nseq, length, D_MODEL), lambda b, t: (b, t, 0)),
        pl.BlockSpec((nseq, HIST_A, D_A), lambda b, t: (b, 0, 0)),
        pl.BlockSpec((nseq, HIST_C, D_C), lambda b, t: (b, 0, 0)),
        per_layer((D_MODEL, D_IN)),
        per_layer((CONV_A_WIDTH, D_A)),
        per_layer((1, D_A)), per_layer((1, D_A)), per_layer((1, D_A)),
        per_layer((1, D_B)), per_layer((1, D_B)),
        per_layer((block_len, N_HEADS_B * block_len)),
        per_layer((block_len, D_B)),
        per_layer((CONV_C_WIDTH, D_C)),
        per_layer((D_MODEL, D_MODEL)),
        per_layer((1, D_MODEL)), per_layer((1, D_MODEL)),
    ]
    out_specs = [
        pl.BlockSpec((nseq, length, D_MODEL), lambda b, t: (b, t, 0)),
        pl.BlockSpec((nseq, HIST_A, D_A), lambda b, t: (b, 0, 0)),
        pl.BlockSpec((nseq, HIST_C, D_C), lambda b, t: (b, 0, 0)),
    ]
    out_shape = [
        jax.ShapeDtypeStruct((batch, seq, D_MODEL), F32),
        jax.ShapeDtypeStruct((batch, HIST_A, D_A), F32),
        jax.ShapeDtypeStruct((batch, HIST_C, D_C), F32),
    ]
    if emit_v:
        out_specs.append(pl.BlockSpec((nseq, length, D_B), lambda b, t: (b, t, 0)))
        out_shape.append(jax.ShapeDtypeStruct((batch, seq, D_B), F32))
    block_bytes = 2 * (2 * rows * D_MODEL * 4 + rows * D_B * 4
                       + D_MODEL * D_IN * 2 + D_MODEL * D_MODEL * 2
                       + block_len * (N_HEADS_B * block_len + D_B) * 4
                       + 64 * D_MODEL * 4)
    scratch_bytes = (nseq * (A_PAD + length) * D_A * 4 + nseq * (C_PAD + length) * D_C * 4
                     + rows * D_MODEL * 2)
    temp_bytes = rows * (D_MODEL * 2 + D_IN * 4 + D_MODEL * 4 * 3)
    return pl.pallas_call(
        functools.partial(_mixer_kernel, nseq=nseq, length=length, block_len=block_len,
                          emit_v=emit_v),
        grid=(batch // nseq, seq // length),
        in_specs=in_specs,
        out_specs=out_specs,
        out_shape=out_shape,
        scratch_shapes=[
            pltpu.VMEM((nseq, A_PAD + length, D_A), F32),
            pltpu.VMEM((nseq, C_PAD + length, D_C), F32),
            pltpu.VMEM((rows, D_MODEL), BF16),
        ],
        compiler_params=pltpu.CompilerParams(
            dimension_semantics=("arbitrary", "arbitrary"),
            vmem_limit_bytes=_vmem_limit(block_bytes, scratch_bytes, temp_bytes)),
        name="mixer",
    )(h, hist_a, hist_c, w_in_b, conv_a_w, conv_a_b, norm_a_g, norm_a_b, norm_b_g, norm_b_b,
      wcat, bias, conv_c_w, w_out_b, ln_g, ln_b)


def _gating_params(w_s, b_s, block_len):
    wcat = w_s[:, :, :block_len, :block_len].transpose(0, 2, 1, 3).reshape(
        DEPTH, block_len, N_HEADS_B * block_len)
    bias = jnp.repeat(b_s[:, :, :block_len].transpose(0, 2, 1), HEAD_DIM, axis=2)
    return wcat, bias


def _interleave_gate_up(w):
    n = D_FF // FF_CHUNK
    return w.reshape(DEPTH, D_MODEL, 2, n, FF_CHUNK).transpose(0, 1, 3, 2, 4).reshape(
        DEPTH, D_MODEL, 2 * D_FF)


def kernel(x_prompt, x_sample, cache_conv_a, cache_conv_c, w_ffn1_in, w_ffn1_out, ln1_g, ln1_b, w_in, conv_a_w, conv_a_b, norm_a_g, norm_a_b, norm_b_g, norm_b_b, w_s, b_s, conv_c_w, w_out, ln2_g, ln2_b, w_ffn2_in, w_ffn2_out, ln3_g, ln3_b):
    bp, seq, _ = x_prompt.shape
    bs, dec_seq, _ = x_sample.shape

    w1i = _interleave_gate_up(w_ffn1_in).astype(BF16)
    w1o = w_ffn1_out.astype(BF16)
    w2i = _interleave_gate_up(w_ffn2_in).astype(BF16)
    w2o = w_ffn2_out.astype(BF16)
    w_in_b = w_in.astype(BF16)
    w_out_b = w_out.astype(BF16)
    row = lambda p: p[:, None, :]
    ln1_g, ln1_b, ln2_g, ln2_b, ln3_g, ln3_b = map(row, (ln1_g, ln1_b, ln2_g, ln2_b, ln3_g, ln3_b))
    conv_a_b, norm_a_g, norm_a_b, norm_b_g, norm_b_b = map(
        row, (conv_a_b, norm_a_g, norm_a_b, norm_b_g, norm_b_b))
    wcat_p, bias_p = _gating_params(w_s, b_s, GMLP_BLOCK)
    wcat_s, bias_s = _gating_params(w_s, b_s, dec_seq)
    zeros_a = jnp.zeros((bp, HIST_A, D_A), F32)
    zeros_c = jnp.zeros((bp, HIST_C, D_C), F32)

    xp = x_prompt.reshape(bp * seq, D_MODEL)
    xs = x_sample.reshape(bs * dec_seq, D_MODEL)
    st_a_p, st_c_p, st_a_s, st_c_s, st_v_s = [], [], [], [], []
    for l in range(DEPTH):
        mixer_args = (w_in_b, conv_a_w, conv_a_b, norm_a_g, norm_a_b, norm_b_g, norm_b_b)
        xp = _ffn(xp, w1i, w1o, ln1_g, ln1_b, l)
        xs = _ffn(xs, w1i, w1o, ln1_g, ln1_b, l)
        xp3, sa, sc = _mixer(xp.reshape(bp, seq, D_MODEL), zeros_a, zeros_c, *mixer_args,
                             wcat_p, bias_p, conv_c_w, w_out_b, ln2_g, ln2_b, l,
                             nseq=1, length=MIX_ROWS, block_len=GMLP_BLOCK, emit_v=False)
        st_a_p.append(sa)
        st_c_p.append(sc)
        xs3, sa, sc, sv = _mixer(xs.reshape(bs, dec_seq, D_MODEL), cache_conv_a[l], cache_conv_c[l],
                                 *mixer_args, wcat_s, bias_s, conv_c_w, w_out_b, ln2_g, ln2_b, l,
                                 nseq=bs, length=dec_seq, block_len=dec_seq, emit_v=True)
        st_a_s.append(sa)
        st_c_s.append(sc)
        st_v_s.append(sv)
        xp = _ffn(xp3.reshape(bp * seq, D_MODEL), w2i, w2o, ln3_g, ln3_b, l)
        xs = _ffn(xs3.reshape(bs * dec_seq, D_MODEL), w2i, w2o, ln3_g, ln3_b, l)
    return (xp.reshape(bp, seq, D_MODEL), xs.reshape(bs, dec_seq, D_MODEL),
            jnp.stack(st_a_p), jnp.stack(st_c_p), jnp.stack(st_a_s), jnp.stack(st_c_s),
            jnp.stack(st_v_s))
```

```python
import functools

import jax
import jax.numpy as jnp
from jax import lax
from jax.experimental import pallas as pl
from jax.experimental.pallas import tpu as pltpu

D_MODEL = 1024
DEPTH = 4
CHUNK = 64
HEAD_DIM = 64
D_A = 384
D_B = 384
D_C = 256
N_HEADS_B = D_B // HEAD_DIM
D_IN = 2 * D_A + 2 * D_B + 3 * D_C
CONV_A_WIDTH = 31
CONV_C_WIDTH = 3
HIST_A = CONV_A_WIDTH - 1
HIST_C = CONV_C_WIDTH - 1
GMLP_BLOCK = 128
D_FF = 2816
ALPHA = (2.0 * DEPTH) ** 0.25
LN_EPS = 1e-5

V7X_SUBLANES = 8
V7X_LANES = 128
V7X_MXU_COLS = 256
V7X_VMEM_BYTES = 64 * 1024 * 1024

FFN_ROWS = 512
FF_CHUNK = V7X_MXU_COLS
MIX_ROWS = 512
CONV_ROWS = 64
A_PAD = 32
C_PAD = 8

F32 = jnp.float32
BF16 = jnp.bfloat16


def _layer_norm(y, g, b):
    mu = jnp.mean(y, axis=-1, keepdims=True)
    d = y - mu
    var = jnp.mean(d * d, axis=-1, keepdims=True)
    return d * lax.rsqrt(var + LN_EPS) * g + b


def _vmem_limit(block_bytes, scratch_bytes, temp_bytes):
    need = block_bytes + scratch_bytes + temp_bytes
    assert need <= V7X_VMEM_BYTES, need
    return need


def _ffn_kernel(x_ref, w_in_ref, w_out_ref, g_ref, b_ref, o_ref, act_ref):
    x = x_ref[...]
    xb = x.astype(BF16)
    for c in range(D_FF // FF_CHUNK):
        cols = slice(c * FF_CHUNK, (c + 1) * FF_CHUNK)
        up_cols = slice(D_FF + c * FF_CHUNK, D_FF + (c + 1) * FF_CHUNK)
        gate = jnp.dot(xb, w_in_ref[:, cols], preferred_element_type=F32)
        up = jnp.dot(xb, w_in_ref[:, up_cols], preferred_element_type=F32)
        act_ref[:, cols] = (jax.nn.silu(gate) * up).astype(BF16)
    y = jnp.dot(act_ref[...], w_out_ref[...], preferred_element_type=F32)
    o_ref[...] = _layer_norm(ALPHA * x + 0.5 * y, g_ref[...], b_ref[...])


def _ffn(x2d, w_in_b, w_out_b, ln_g, ln_b, layer):
    rows = x2d.shape[0]
    tm = min(FFN_ROWS, rows)
    assert rows % tm == 0
    resident = dict(pipeline_mode=pl.Buffered(1))
    block_bytes = (2 * 2 * tm * D_MODEL * 4
                   + D_MODEL * 2 * D_FF * 2 + D_FF * D_MODEL * 2
                   + 2 * 2 * D_MODEL * 4)
    scratch_bytes = tm * D_FF * 2
    temp_bytes = tm * (D_MODEL * 2 + 2 * FF_CHUNK * 4 * 3 + D_MODEL * 4 * 3)
    return pl.pallas_call(
        _ffn_kernel,
        grid=(rows // tm,),
        in_specs=[
            pl.BlockSpec((tm, D_MODEL), lambda i: (i, 0)),
            pl.BlockSpec((None, D_MODEL, 2 * D_FF), lambda i: (layer, 0, 0), **resident),
            pl.BlockSpec((None, D_FF, D_MODEL), lambda i: (layer, 0, 0), **resident),
            pl.BlockSpec((None, 1, D_MODEL), lambda i: (layer, 0, 0)),
            pl.BlockSpec((None, 1, D_MODEL), lambda i: (layer, 0, 0)),
        ],
        out_specs=pl.BlockSpec((tm, D_MODEL), lambda i: (i, 0)),
        out_shape=jax.ShapeDtypeStruct((rows, D_MODEL), F32),
        scratch_shapes=[pltpu.VMEM((tm, D_FF), BF16)],
        compiler_params=pltpu.CompilerParams(
            dimension_semantics=("arbitrary",),
            vmem_limit_bytes=_vmem_limit(block_bytes, scratch_bytes, temp_bytes)),
        name="ffn",
    )(x2d, w_in_b, w_out_b, ln_g, ln_b)


def _mixer_kernel(h_ref, ha0_ref, hc0_ref, w_in_ref, cw_a_ref, cb_a_ref, na_g_ref, na_b_ref,
                  nb_g_ref, nb_b_ref, wcat_ref, bias_ref, cw_c_ref, w_out_ref, g2_ref, b2_ref,
                  *rest, nseq, length, block_len, emit_v):
    if emit_v:
        o_ref, sa_ref, sc_ref, v_ref, abuf, oabuf, cbuf, mix_ref = rest
    else:
        o_ref, sa_ref, sc_ref, abuf, oabuf, cbuf, mix_ref = rest
    rows = nseq * length
    a_lo = A_PAD - HIST_A
    c_lo = C_PAD - HIST_C
    a_cols = [slice(c * V7X_LANES, (c + 1) * V7X_LANES) for c in range(D_A // V7X_LANES)]

    @pl.when(pl.program_id(1) == 0)
    def _():
        for c, cols in enumerate(a_cols):
            abuf[:, c, a_lo:A_PAD, :] = ha0_ref[:, :, cols]
        cbuf[:, c_lo:C_PAD, :] = hc0_ref[...]

    @pl.when(pl.program_id(1) != 0)
    def _():
        abuf[:, :, a_lo:A_PAD, :] = abuf[:, :, length + a_lo:length + A_PAD, :]
        cbuf[:, c_lo:C_PAD, :] = cbuf[:, length + c_lo:length + C_PAD, :]

    h = h_ref[...].reshape(rows, D_MODEL)
    hb = h.astype(BF16)

    za = jnp.dot(hb, w_in_ref[:, 0:2 * D_A], preferred_element_type=F32)
    a = (za[:, :D_A] * jax.nn.sigmoid(za[:, D_A:])).reshape(nseq, length, D_A)
    for c, cols in enumerate(a_cols):
        abuf[:, c, A_PAD:A_PAD + length, :] = a[:, :, cols]
        sa_ref[:, :, cols] = abuf[:, c, length + a_lo:length + A_PAD, :]
    phases = V7X_SUBLANES
    cb_a = [jnp.broadcast_to(cb_a_ref[:, cols], (CONV_ROWS // phases, V7X_LANES)) for cols in a_cols]
    for s in range(nseq):
        for r0 in range(0, length, CONV_ROWS):
            acc = [[cb_a[c]] * phases for c in range(len(a_cols))]
            for c, cols in enumerate(a_cols):
                for j in range(phases + CONV_A_WIDTH - 1):
                    xj = abuf[s, c, pl.ds(a_lo + r0 + j, CONV_ROWS // phases, stride=phases), :]
                    for p in range(phases):
                        k = j - p
                        if 0 <= k < CONV_A_WIDTH:
                            acc[c][p] = acc[c][p] + xj * cw_a_ref[k:k + 1, cols]
            conv = jnp.concatenate(
                [jnp.concatenate([acc[c][p] for c in range(len(a_cols))], axis=1)
                 for p in range(phases)], axis=0)
            out_a = jax.nn.silu(_layer_norm(conv, na_g_ref[...], na_b_ref[...]))
            for p in range(phases):
                for c, cols in enumerate(a_cols):
                    oabuf[c, pl.ds(s * length + r0 + p, CONV_ROWS // phases, stride=phases), :] = (
                        out_a[p * (CONV_ROWS // phases):(p + 1) * (CONV_ROWS // phases), cols])
    for c, cols in enumerate(a_cols):
        mix_ref[:, cols] = oabuf[c].astype(BF16)

    zb = jnp.dot(hb, w_in_ref[:, 2 * D_A:2 * D_A + 2 * D_B], preferred_element_type=F32)
    vn = _layer_norm(zb[:, D_B:], nb_g_ref[...], nb_b_ref[...])
    if emit_v:
        v_ref[...] = vn.reshape(nseq, length, D_B)
    wi = lax.broadcasted_iota(jnp.int32, (block_len, N_HEADS_B * block_len), 0)
    wj = lax.broadcasted_iota(jnp.int32, (block_len, N_HEADS_B * block_len), 1) % block_len
    wm = jnp.where(wj // CHUNK <= wi // CHUNK, wcat_ref[...], 0.0).astype(BF16)
    lane_head = lax.broadcasted_iota(jnp.int32, (block_len, D_B), 1) // HEAD_DIM
    for blk in range(rows // block_len):
        lo = blk * block_len
        vb = vn[lo:lo + block_len].astype(BF16)
        vstack = jnp.concatenate(
            [jnp.where(lane_head == hh, vb, jnp.zeros_like(vb)) for hh in range(N_HEADS_B)], axis=0)
        sg = jnp.dot(wm, vstack, preferred_element_type=F32) + bias_ref[...]
        mix_ref[lo:lo + block_len, D_A:D_A + D_B] = (zb[lo:lo + block_len, :D_B] * sg).astype(BF16)

    zc = jnp.dot(hb, w_in_ref[:, 2 * D_A + 2 * D_B:D_IN], preferred_element_type=F32)
    cx = zc[:, 2 * D_C:] * zc[:, :D_C]
    cbuf[:, C_PAD:C_PAD + length, :] = cx.reshape(nseq, length, D_C)
    sc_ref[...] = cbuf[:, length + c_lo:length + C_PAD, :]
    conv_c = cbuf[:, c_lo:c_lo + length, :] * cw_c_ref[0:1, :]
    for k in range(1, CONV_C_WIDTH):
        conv_c = conv_c + cbuf[:, c_lo + k:c_lo + k + length, :] * cw_c_ref[k:k + 1, :]
    out_c = zc[:, D_C:2 * D_C] * conv_c.reshape(rows, D_C)
    mix_ref[:, D_A + D_B:D_MODEL] = out_c.astype(BF16)

    y = jnp.dot(mix_ref[...], w_out_ref[...], preferred_element_type=F32)
    o = _layer_norm(ALPHA * h + y, g2_ref[...], b2_ref[...])
    o_ref[...] = o.reshape(nseq, length, D_MODEL)


def _mixer(h, hist_a, hist_c, w_in_b, conv_a_w, conv_a_b, norm_a_g, norm_a_b, norm_b_g, norm_b_b,
           wcat, bias, conv_c_w, w_out_b, ln_g, ln_b, layer, *, nseq, length, block_len, emit_v):
    batch, seq, _ = h.shape
    assert batch % nseq == 0 and seq % length == 0 and length % block_len == 0
    assert length % CONV_ROWS == 0 and length >= HIST_A
    rows = nseq * length

    def per_layer(shape):
        return pl.BlockSpec((None,) + shape, lambda b, t: (layer,) + (0,) * len(shape))

    in_specs = [
        pl.BlockSpec((nseq, length, D_MODEL), lambda b, t: (b, t, 0)),
        pl.BlockSpec((nseq, HIST_A, D_A), lambda b, t: (b, 0, 0)),
        pl.BlockSpec((nseq, HIST_C, D_C), lambda b, t: (b, 0, 0)),
        per_layer((D_MODEL, D_IN)),
        per_layer((CONV_A_WIDTH, D_A)),
        per_layer((1, D_A)), per_layer((1, D_A)), per_layer((1, D_A)),
        per_layer((1, D_B)), per_layer((1, D_B)),
        per_layer((block_len, N_HEADS_B * block_len)),
        per_layer((block_len, D_B)),
        per_layer((CONV_C_WIDTH, D_C)),
        per_layer((D_MODEL, D_MODEL)),
        per_layer((1, D_MODEL)), per_layer((1, D_MODEL)),
    ]
    out_specs = [
        pl.BlockSpec((nseq, length, D_MODEL), lambda b, t: (b, t, 0)),
        pl.BlockSpec((nseq, HIST_A, D_A), lambda b, t: (b, 0, 0)),
        pl.BlockSpec((nseq, HIST_C, D_C), lambda b, t: (b, 0, 0)),
    ]
    out_shape = [
        jax.ShapeDtypeStruct((batch, seq, D_MODEL), F32),
        jax.ShapeDtypeStruct((batch, HIST_A, D_A), F32),
        jax.ShapeDtypeStruct((batch, HIST_C, D_C), F32),
    ]
    if emit_v:
        out_specs.append(pl.BlockSpec((nseq, length, D_B), lambda b, t: (b, t, 0)))
        out_shape.append(jax.ShapeDtypeStruct((batch, seq, D_B), F32))
    block_bytes = 2 * (2 * rows * D_MODEL * 4 + rows * D_B * 4
                       + D_MODEL * D_IN * 2 + D_MODEL * D_MODEL * 2
                       + block_len * (N_HEADS_B * block_len + D_B) * 4
                       + 64 * D_MODEL * 4)
    scratch_bytes = (nseq * (A_PAD + length) * D_A * 4 + rows * D_A * 4
                     + nseq * (C_PAD + length) * D_C * 4 + rows * D_MODEL * 2)
    temp_bytes = rows * (D_MODEL * 2 + D_IN * 4 + D_MODEL * 4 * 3)
    return pl.pallas_call(
        functools.partial(_mixer_kernel, nseq=nseq, length=length, block_len=block_len,
                          emit_v=emit_v),
        grid=(batch // nseq, seq // length),
        in_specs=in_specs,
        out_specs=out_specs,
        out_shape=out_shape,
        scratch_shapes=[
            pltpu.VMEM((nseq, D_A // V7X_LANES, A_PAD + length, V7X_LANES), F32),
            pltpu.VMEM((D_A // V7X_LANES, rows, V7X_LANES), F32),
            pltpu.VMEM((nseq, C_PAD + length, D_C), F32),
            pltpu.VMEM((rows, D_MODEL), BF16),
        ],
        compiler_params=pltpu.CompilerParams(
            dimension_semantics=("arbitrary", "arbitrary"),
            vmem_limit_bytes=_vmem_limit(block_bytes, scratch_bytes, temp_bytes)),
        name="mixer",
    )(h, hist_a, hist_c, w_in_b, conv_a_w, conv_a_b, norm_a_g, norm_a_b, norm_b_g, norm_b_b,
      wcat, bias, conv_c_w, w_out_b, ln_g, ln_b)


def _gating_params(w_s, b_s, block_len):
    wcat = w_s[:, :, :block_len, :block_len].transpose(0, 2, 1, 3).reshape(
        DEPTH, block_len, N_HEADS_B * block_len)
    bias = jnp.repeat(b_s[:, :, :block_len].transpose(0, 2, 1), HEAD_DIM, axis=2)
    return wcat, bias


def kernel(x_prompt, x_sample, cache_conv_a, cache_conv_c, w_ffn1_in, w_ffn1_out, ln1_g, ln1_b, w_in, conv_a_w, conv_a_b, norm_a_g, norm_a_b, norm_b_g, norm_b_b, w_s, b_s, conv_c_w, w_out, ln2_g, ln2_b, w_ffn2_in, w_ffn2_out, ln3_g, ln3_b):
    bp, seq, _ = x_prompt.shape
    bs, dec_seq, _ = x_sample.shape

    w1i = w_ffn1_in.astype(BF16)
    w1o = w_ffn1_out.astype(BF16)
    w2i = w_ffn2_in.astype(BF16)
    w2o = w_ffn2_out.astype(BF16)
    w_in_b = w_in.astype(BF16)
    w_out_b = w_out.astype(BF16)
    row = lambda p: p[:, None, :]
    ln1_g, ln1_b, ln2_g, ln2_b, ln3_g, ln3_b = map(row, (ln1_g, ln1_b, ln2_g, ln2_b, ln3_g, ln3_b))
    conv_a_b, norm_a_g, norm_a_b, norm_b_g, norm_b_b = map(
        row, (conv_a_b, norm_a_g, norm_a_b, norm_b_g, norm_b_b))
    wcat_p, bias_p = _gating_params(w_s, b_s, GMLP_BLOCK)
    wcat_s, bias_s = _gating_params(w_s, b_s, dec_seq)
    zeros_a = jnp.zeros((bp, HIST_A, D_A), F32)
    zeros_c = jnp.zeros((bp, HIST_C, D_C), F32)

    xp = x_prompt.reshape(bp * seq, D_MODEL)
    xs = x_sample.reshape(bs * dec_seq, D_MODEL)
    st_a_p, st_c_p, st_a_s, st_c_s, st_v_s = [], [], [], [], []
    for l in range(DEPTH):
        mixer_args = (w_in_b, conv_a_w, conv_a_b, norm_a_g, norm_a_b, norm_b_g, norm_b_b)
        xp = _ffn(xp, w1i, w1o, ln1_g, ln1_b, l)
        xs = _ffn(xs, w1i, w1o, ln1_g, ln1_b, l)
        xp3, sa, sc = _mixer(xp.reshape(bp, seq, D_MODEL), zeros_a, zeros_c, *mixer_args,
                             wcat_p, bias_p, conv_c_w, w_out_b, ln2_g, ln2_b, l,
                             nseq=1, length=MIX_ROWS, block_len=GMLP_BLOCK, emit_v=False)
        st_a_p.append(sa)
        st_c_p.append(sc)
        xs3, sa, sc, sv = _mixer(xs.reshape(bs, dec_seq, D_MODEL), cache_conv_a[l], cache_conv_c[l],
                                 *mixer_args, wcat_s, bias_s, conv_c_w, w_out_b, ln2_g, ln2_b, l,
                                 nseq=bs, length=dec_seq, block_len=dec_seq, emit_v=True)
        st_a_s.append(sa)
        st_c_s.append(sc)
        st_v_s.append(sv)
        xp = _ffn(xp3.reshape(bp * seq, D_MODEL), w2i, w2o, ln3_g, ln3_b, l)
        xs = _ffn(xs3.reshape(bs * dec_seq, D_MODEL), w2i, w2o, ln3_g, ln3_b, l)
    return (xp.reshape(bp, seq, D_MODEL), xs.reshape(bs, dec_seq, D_MODEL),
            jnp.stack(st_a_p), jnp.stack(st_c_p), jnp.stack(st_a_s), jnp.stack(st_c_s),
            jnp.stack(st_v_s))
```
